```python
import math
import jax
import jax.numpy as jnp
from jax import lax
import numpy as np

D_MODEL = 1024
BATCH = 16
SEQ = 4096
DEPTH = 4
DEC_BATCH = 8
DEC_SEQ = 2048
PAST_LEN = 128

N_BRANCH = 4
BR_W = D_MODEL // N_BRANCH
HY_W = BR_W
HY_ORDER = 2
HY_SHORT = 3
HY_BANDS = 16
HY_EMB = 1 + 2 * HY_BANDS
HY_HID = 64
HY_MIN_DECAY = math.log(1e-2) / 1.5
HY_MAX_DECAY = math.log(1e-2) / 0.3
DA_HEADS = 4
DA_HD = BR_W // (2 * DA_HEADS)
QBLK = 128
RET_HEADS = 4
RET_DK = BR_W // (2 * RET_HEADS)
RET_DV = BR_W // RET_HEADS
SSM_HEADS = 4
SSM_P = BR_W // SSM_HEADS
SSM_GROUPS = 2
SSM_N = 64
SSM_CONV = 5
SSM_XBC = BR_W + 2 * SSM_GROUPS * SSM_N
CHUNK = 128
PEER_HEADS = 8
PEER_NK = 128
PEER_N = PEER_NK * PEER_NK
PEER_DK = 128
PEER_TOPK = 16
PEER_TBLK = 128
HY_COLS = 3 * HY_W
DA_COLS = 3 * BR_W
RET_COLS = 2 * RET_HEADS * RET_DK + 2 * BR_W
SSM_COLS = BR_W + SSM_XBC + 2 * SSM_HEADS
IN_COLS = HY_COLS + DA_COLS + RET_COLS + SSM_COLS
IN_SPLITS = (HY_COLS, HY_COLS + DA_COLS, HY_COLS + DA_COLS + RET_COLS)

kernel_name = 'hybrid_bidir_hyena_diffattn_retnet_ssd_peer'


def rmsnorm(x, w, eps=1e-6):
    xf = x.astype(jnp.float32)
    y = xf * lax.rsqrt(jnp.mean(xf * xf, axis=-1, keepdims=True) + eps)
    return (y * w).astype(x.dtype)


def centred_dwconv(x, w, b):
    K, C = w.shape
    y = lax.conv_general_dilated(x, w[:, None, :], window_strides=(1,), padding=[(K // 2, K // 2)],
                                 dimension_numbers=('NWC', 'WIO', 'NWC'), feature_group_count=C)
    return y + b


def chunk_scan(q, k, v, log_a):
    Bsz, L, H, N = q.shape
    P = v.shape[-1]
    nc = L // CHUNK
    qc = q.reshape(Bsz, nc, CHUNK, H, N)
    kc = k.reshape(Bsz, nc, CHUNK, H, N)
    vc = v.reshape(Bsz, nc, CHUNK, H, P)
    cum = jnp.cumsum(log_a.astype(jnp.float32).reshape(Bsz, nc, CHUNK, H), axis=2)
    cum_h = jnp.moveaxis(cum, 3, 2)
    causal = jnp.tril(jnp.ones((CHUNK, CHUNK), dtype=bool))
    seg = jnp.where(causal, cum_h[..., :, None] - cum_h[..., None, :], -jnp.inf)
    scores = jnp.einsum('bcihn,bcjhn->bchij', qc, kc) * jnp.exp(seg)
    y = jnp.einsum('bchij,bcjhp->bcihp', scores, vc)
    to_end = jnp.exp(cum[:, :, -1:, :] - cum)
    chunk_states = jnp.einsum('bcjhn,bcjh,bcjhp->bchnp', kc, to_end, vc)
    chunk_decay = jnp.exp(cum[:, :, -1, :])

    def step(h, inp):
        s, d = inp
        return h * d[..., None, None] + s, h

    h0 = jnp.zeros((Bsz, H, N, P), chunk_states.dtype)
    _, h_prev = lax.scan(step, h0, (jnp.moveaxis(chunk_states, 1, 0), jnp.moveaxis(chunk_decay, 1, 0)))
    h_prev = jnp.moveaxis(h_prev, 0, 1)
    y = y + jnp.einsum('bcihn,bcih,bchnp->bcihp', qc, jnp.exp(cum), h_prev)
    return y.reshape(Bsz, L, H, P)


def bidir_scan(q, k_fwd, k_bwd, v, la_fwd, la_bwd):
    fl = lambda a: jnp.flip(a, axis=1)
    return chunk_scan(q, k_fwd, v, la_fwd) + fl(chunk_scan(fl(q), fl(k_bwd), fl(v), fl(la_bwd)))


def hyena_filters(L, w1, b1, w2, b2, w3, freq):
    f32 = jnp.float32
    pos = jnp.arange(L, dtype=f32)
    t = pos / (L - 1)
    bands = jnp.linspace(1e-4, HY_BANDS - 1, HY_BANDS, dtype=f32)
    ang = (2.0 * math.pi / L) * pos[:, None] * bands[None, :]
    emb = jnp.concatenate([t[:, None], jnp.cos(ang), -jnp.sin(ang)], axis=-1)
    fr = freq.astype(f32)
    hid = jnp.sin(fr * (emb @ w1.astype(f32) + b1.astype(f32)))
    hid = jnp.sin(fr * (hid @ w2.astype(f32) + b2.astype(f32)))
    filt = hid @ w3.astype(f32)
    deltas = jnp.abs(jnp.linspace(HY_MIN_DECAY, HY_MAX_DECAY, HY_W, dtype=f32))
    window = jnp.exp(-t[:, None] * jnp.tile(deltas, HY_ORDER * 2)[None, :])
    filt = filt * window
    filt = filt * lax.rsqrt(jnp.sum(filt * filt, axis=0, keepdims=True) + 1e-6)
    return filt.reshape(L, HY_ORDER, 2, HY_W)


def bidir_fftconv(z, h_fwd, h_bwd, skip):
    L, C = h_fwd.shape
    kern = jnp.concatenate([h_fwd, jnp.zeros((1, C), h_fwd.dtype), h_bwd[:0:-1]], axis=0)
    kf = jnp.fft.rfft(kern, n=2 * L, axis=0)
    zf32 = z.astype(jnp.float32)
    zf = jnp.fft.rfft(zf32, n=2 * L, axis=1)
    y = jnp.fft.irfft(zf * kf[None], n=2 * L, axis=1)[:, :L]
    return (y + zf32 * skip.astype(jnp.float32)).astype(z.dtype)


def hyena_branch(p, P, l):
    L = p.shape[1]
    u = centred_dwconv(p, P['hy_conv_w'][l], P['hy_conv_b'][l])
    v, x1, x2 = jnp.split(u, 3, axis=-1)
    filt = hyena_filters(L, P['hf_w1'][l], P['hf_b1'][l], P['hf_w2'][l], P['hf_b2'][l], P['hf_w3'][l], P['hf_freq'][l])
    z = v
    for o, gate in enumerate((x1, x2)):
        z = gate * bidir_fftconv(z, filt[:, o, 0], filt[:, o, 1], P['hy_bias'][l, o])
    return z


def diffattn_branch(p, P, l):
    f32 = jnp.float32
    B, L, _ = p.shape
    q, k, v = jnp.split(p, 3, axis=-1)
    q = rmsnorm(q.reshape(B, L, DA_HEADS, 2, DA_HD), P['da_qnorm_w'][l]) * (DA_HD ** -0.5)
    k = rmsnorm(k.reshape(B, L, DA_HEADS, 2, DA_HD), P['da_knorm_w'][l])
    v = v.reshape(B, L, DA_HEADS, 2 * DA_HD)
    lam_init = 0.8 - 0.6 * math.exp(-0.3 * l)
    lam = (jnp.exp(jnp.sum(P['da_lq1'][l].astype(f32) * P['da_lk1'][l].astype(f32)))
           - jnp.exp(jnp.sum(P['da_lq2'][l].astype(f32) * P['da_lk2'][l].astype(f32))) + lam_init)
    nblk = L // QBLK
    qb = q.reshape(B, nblk, QBLK, DA_HEADS, 2, DA_HD).transpose(1, 0, 3, 4, 2, 5)
    kt = k.transpose(0, 2, 3, 1, 4)
    vt = v.transpose(0, 2, 1, 3)
    slopes = 2.0 ** (-8.0 * (jnp.arange(DA_HEADS, dtype=f32) + 1.0) / DA_HEADS)
    kpos = jnp.arange(L, dtype=f32)

    def block(args):
        qblk, start = args
        qpos = start.astype(f32) + jnp.arange(QBLK, dtype=f32)
        bias = -slopes[:, None, None] * jnp.abs(qpos[:, None] - kpos[None, :])
        s = jnp.einsum('bhcqd,bhckd->bhcqk', qblk, kt).astype(f32) + bias[None, :, None]
        pr = jax.nn.softmax(s, axis=-1)
        a = pr[:, :, 0] - lam * pr[:, :, 1]
        return jnp.einsum('bhqk,bhkd->bhqd', a.astype(vt.dtype), vt)

    o = lax.map(block, (qb, jnp.arange(nblk) * QBLK))
    o = o.transpose(1, 0, 3, 2, 4).reshape(B, L, DA_HEADS, 2 * DA_HD)
    o = rmsnorm(o, P['da_subln_w'][l].reshape(DA_HEADS, 2 * DA_HD)) * (1.0 - lam_init)
    return o.reshape(B, L, BR_W)


def retention_branch(p, P, l):
    B, L, _ = p.shape
    qk = RET_HEADS * RET_DK
    q, k, v, g = jnp.split(p, [qk, 2 * qk, 2 * qk + BR_W], axis=-1)
    q = q.reshape(B, L, RET_HEADS, RET_DK)
    k = k.reshape(B, L, RET_HEADS, RET_DK) * (RET_DK ** -0.5)
    v = v.reshape(B, L, RET_HEADS, RET_DV)
    hidx = jnp.arange(RET_HEADS, dtype=jnp.float32)
    la_f = jnp.broadcast_to(jnp.log1p(-(2.0 ** (-5.0 - hidx))), (B, L, RET_HEADS))
    la_b = jnp.broadcast_to(jnp.log1p(-(2.0 ** (-5.5 - hidx))), (B, L, RET_HEADS))
    y = bidir_scan(q, k, k, v, la_f, la_b)
    y = rmsnorm(y, P['ret_gn_w'][l].reshape(RET_HEADS, RET_DV)).reshape(B, L, BR_W).astype(p.dtype)
    return jax.nn.silu(g) * y


def ssd_branch(p, P, l):
    B, L, _ = p.shape
    z, xbc, dt = jnp.split(p, [BR_W, BR_W + SSM_XBC], axis=-1)
    xbc = jax.nn.silu(centred_dwconv(xbc, P['ssm_conv_w'][l], P['ssm_conv_b'][l]))
    xs, Bm, Cm = jnp.split(xbc, [BR_W, BR_W + SSM_GROUPS * SSM_N], axis=-1)
    xs = xs.reshape(B, L, SSM_HEADS, SSM_P)
    rep = SSM_HEADS // SSM_GROUPS
    Bh = jnp.repeat(Bm.reshape(B, L, SSM_GROUPS, SSM_N), rep, axis=2)
    Ch = jnp.repeat(Cm.reshape(B, L, SSM_GROUPS, SSM_N), rep, axis=2)
    dt = jax.nn.softplus(dt.astype(jnp.float32).reshape(B, L, 2, SSM_HEADS) + P['ssm_dt_bias'][l])
    A = -jnp.exp(P['ssm_A_log'][l].astype(jnp.float32))
    y = bidir_scan(Ch, Bh * dt[:, :, 0, :, None], Bh * dt[:, :, 1, :, None], xs,
                   dt[:, :, 0] * A[0], dt[:, :, 1] * A[1])
    y = y + P['ssm_D'][l][:, None] * xs
    y = y.reshape(B, L, BR_W).astype(p.dtype) * jax.nn.silu(z)
    return rmsnorm(y, P['ssm_norm_w'][l])


def peer_ffn(h, wq, bq, keys, u, v):
    B, L, D = h.shape
    T = B * L
    ht = h.reshape(T, D)
    q = (ht @ wq + bq).reshape(T, PEER_HEADS, 2, PEER_DK)
    s = jnp.einsum('thcd,hckd->thck', q, keys).astype(jnp.float32)
    s1, i1 = lax.top_k(s[:, :, 0], PEER_TOPK)
    s2, i2 = lax.top_k(s[:, :, 1], PEER_TOPK)
    cand_s = (s1[..., :, None] + s2[..., None, :]).reshape(T, PEER_HEADS, PEER_TOPK * PEER_TOPK)
    cand_i = (i1[..., :, None] * PEER_NK + i2[..., None, :]).reshape(T, PEER_HEADS, PEER_TOPK * PEER_TOPK)
    top_s, pos = lax.top_k(cand_s, PEER_TOPK)
    idx = jnp.take_along_axis(cand_i, pos, axis=-1)
    g = jax.nn.softmax(top_s, axis=-1)
    nb = T // PEER_TBLK
    E = PEER_HEADS * PEER_TOPK

    def block(args):
        xb, ib, gb = args
        act = jax.nn.gelu(jnp.einsum('td,ted->te', xb, u[ib]).astype(jnp.float32), approximate=False)
        return jnp.einsum('te,ted->td', (gb * act).astype(xb.dtype), v[ib])

    out = lax.map(block, (ht.reshape(nb, PEER_TBLK, D), idx.reshape(nb, PEER_TBLK, E), g.reshape(nb, PEER_TBLK, E)))
    return out.reshape(B, L, D)


def encoder_layer(x, c, P, l):
    ada = jax.nn.silu(c) @ P['ada_w'][l] + P['ada_b'][l]
    sh1, sc1, g1, sh2, sc2, g2 = jnp.split(ada[:, None, :], 6, axis=-1)
    h = rmsnorm(x, P['norm1_w'][l]) * (1.0 + sc1) + sh1
    proj = h @ P['w_in'][l]
    p_hy, p_da, p_ret, p_ssm = jnp.split(proj, IN_SPLITS, axis=-1)
    ys = (hyena_branch(p_hy, P, l), diffattn_branch(p_da, P, l),
          retention_branch(p_ret, P, l), ssd_branch(p_ssm, P, l))
    merged = 0.0
    for i in range(N_BRANCH):
        gate = jax.nn.sigmoid(h @ P['w_gate'][l, i] + P['b_gate'][l, i])
        merged = merged + gate * (ys[i] @ P['w_branch'][l, i])
    x = x + g1 * (merged @ P['w_out'][l])
    h2 = rmsnorm(x, P['norm2_w'][l]) * (1.0 + sc2) + sh2
    x = x + g2 * peer_ffn(h2, P['peer_wq'][l], P['peer_bq'][l], P['peer_keys'][l], P['peer_u'][l], P['peer_v'][l])
    return x


def trunk(x, c, P):
    for l in range(DEPTH):
        x = encoder_layer(x, c, P, l)
    return x


def setup_inputs(seed: int = 0) -> dict:
    key = jax.random.key(seed)
    ks = list(jax.random.split(key, 64))

    def nrm(shape, scale):
        return jax.random.normal(ks.pop(), shape, jnp.float32) * scale

    def gain(shape):
        return 1.0 + nrm(shape, 0.01)

    D = D_MODEL
    dt = jnp.exp(jax.random.uniform(ks.pop(), (DEPTH, 2, SSM_HEADS), jnp.float32, math.log(1e-3), math.log(1e-1)))
    a_init = jax.random.uniform(ks.pop(), (DEPTH, 2, SSM_HEADS), jnp.float32, 1.0, 16.0)
    return {
        'x_prompt': nrm((BATCH, SEQ, D), 1.0),
        'x_sample': nrm((DEC_BATCH, DEC_SEQ, D), 1.0),
        'c_prompt': nrm((BATCH, D), 1.0),
        'c_sample': nrm((DEC_BATCH, D), 1.0),
        'ada_w': nrm((DEPTH, D, 6 * D), D ** -0.5),
        'ada_b': nrm((DEPTH, 6 * D), 0.01),
        'norm1_w': gain((DEPTH, D)),
        'norm2_w': gain((DEPTH, D)),
        'w_in': nrm((DEPTH, D, IN_COLS), D ** -0.5),
        'hy_conv_w': nrm((DEPTH, HY_SHORT, HY_COLS), HY_SHORT ** -0.5),
        'hy_conv_b': nrm((DEPTH, HY_COLS), 0.01),
        'hf_w1': nrm((DEPTH, HY_EMB, HY_HID), HY_EMB ** -0.5),
        'hf_b1': nrm((DEPTH, HY_HID), 0.1),
        'hf_w2': nrm((DEPTH, HY_HID, HY_HID), HY_HID ** -0.5),
        'hf_b2': nrm((DEPTH, HY_HID), 0.1),
        'hf_w3': nrm((DEPTH, HY_HID, HY_ORDER * 2 * HY_W), HY_HID ** -0.5),
        'hf_freq': gain((DEPTH, HY_HID)),
        'hy_bias': nrm((DEPTH, HY_ORDER, HY_W), 1.0),
        'da_qnorm_w': gain((DEPTH, DA_HD)),
        'da_knorm_w': gain((DEPTH, DA_HD)),
        'da_lq1': nrm((DEPTH, DA_HD), 0.1),
        'da_lk1': nrm((DEPTH, DA_HD), 0.1),
        'da_lq2': nrm((DEPTH, DA_HD), 0.1),
        'da_lk2': nrm((DEPTH, DA_HD), 0.1),
        'da_subln_w': gain((DEPTH, BR_W)),
        'ret_gn_w': gain((DEPTH, BR_W)),
        'ssm_conv_w': nrm((DEPTH, SSM_CONV, SSM_XBC), SSM_CONV ** -0.5),
        'ssm_conv_b': nrm((DEPTH, SSM_XBC), 0.01),
        'ssm_dt_bias': dt + jnp.log(-jnp.expm1(-dt)),
        'ssm_A_log': jnp.log(a_init),
        'ssm_D': gain((DEPTH, SSM_HEADS)),
        'ssm_norm_w': gain((DEPTH, BR_W)),
        'w_branch': nrm((DEPTH, N_BRANCH, BR_W, D), BR_W ** -0.5),
        'w_gate': nrm((DEPTH, N_BRANCH, D, D), D ** -0.5),
        'b_gate': nrm((DEPTH, N_BRANCH, D), 0.01),
        'w_out': nrm((DEPTH, D, D), D ** -0.5),
        'peer_wq': nrm((DEPTH, D, PEER_HEADS * 2 * PEER_DK), D ** -0.5),
        'peer_bq': nrm((DEPTH, PEER_HEADS * 2 * PEER_DK), 0.01),
        'peer_keys': nrm((DEPTH, PEER_HEADS, 2, PEER_NK, PEER_DK), PEER_DK ** -0.5),
        'peer_u': nrm((DEPTH, PEER_N, D), D ** -0.5),
        'peer_v': nrm((DEPTH, PEER_N, D), 0.5),
    }


def reference(x_prompt, x_sample, c_prompt, c_sample, ada_w, ada_b, norm1_w, norm2_w, w_in,
              hy_conv_w, hy_conv_b, hf_w1, hf_b1, hf_w2, hf_b2, hf_w3, hf_freq, hy_bias,
              da_qnorm_w, da_knorm_w, da_lq1, da_lk1, da_lq2, da_lk2, da_subln_w, ret_gn_w,
              ssm_conv_w, ssm_conv_b, ssm_dt_bias, ssm_A_log, ssm_D, ssm_norm_w,
              w_branch, w_gate, b_gate, w_out, peer_wq, peer_bq, peer_keys, peer_u, peer_v):
    P = dict(ada_w=ada_w, ada_b=ada_b, norm1_w=norm1_w, norm2_w=norm2_w, w_in=w_in,
             hy_conv_w=hy_conv_w, hy_conv_b=hy_conv_b, hf_w1=hf_w1, hf_b1=hf_b1, hf_w2=hf_w2, hf_b2=hf_b2,
             hf_w3=hf_w3, hf_freq=hf_freq, hy_bias=hy_bias, da_qnorm_w=da_qnorm_w, da_knorm_w=da_knorm_w,
             da_lq1=da_lq1, da_lk1=da_lk1, da_lq2=da_lq2, da_lk2=da_lk2, da_subln_w=da_subln_w,
             ret_gn_w=ret_gn_w, ssm_conv_w=ssm_conv_w, ssm_conv_b=ssm_conv_b, ssm_dt_bias=ssm_dt_bias,
             ssm_A_log=ssm_A_log, ssm_D=ssm_D, ssm_norm_w=ssm_norm_w, w_branch=w_branch, w_gate=w_gate,
             b_gate=b_gate, w_out=w_out, peer_wq=peer_wq, peer_bq=peer_bq, peer_keys=peer_keys,
             peer_u=peer_u, peer_v=peer_v)
    y_prompt = trunk(x_prompt, c_prompt, P)
    y_sample = trunk(x_sample, c_sample, P)
    return (y_prompt, y_sample)
```

```python
import functools
import math

import jax
import jax.numpy as jnp
import numpy as np
from jax import lax
from jax.experimental import pallas as pl
from jax.experimental.pallas import tpu as pltpu

f32 = jnp.float32
bf16 = jnp.bfloat16

LANES = 128
SUBLANES = 8
VMEM_LIMIT_BYTES = 48 * 1024 * 1024

D_MODEL = 1024
DEPTH = 4
N_BRANCH = 4
BR_W = D_MODEL // N_BRANCH
HY_BANDS = 16
HY_EMB = 1 + 2 * HY_BANDS
HY_HID = 64
HY_ORDER = 2
HY_MIN_DECAY = math.log(1e-2) / 1.5
HY_MAX_DECAY = math.log(1e-2) / 0.3
DA_HEADS = 4
DA_HD = BR_W // (2 * DA_HEADS)
RET_HEADS = 4
RET_DK = BR_W // (2 * RET_HEADS)
RET_DV = BR_W // RET_HEADS
SSM_HEADS = 4
SSM_P = BR_W // SSM_HEADS
SSM_GROUPS = 2
SSM_N = 64
SSM_XBC = BR_W + 2 * SSM_GROUPS * SSM_N
CHUNK = 128
PEER_HEADS = 8
PEER_NK = 128
PEER_N = PEER_NK * PEER_NK
PEER_DK = 128
PEER_TOPK = 16
IN_COLS_PAD = 3 * 768 + 768 + LANES
EPS = 1e-6


def _cparams(*sem):
    return pltpu.CompilerParams(dimension_semantics=sem, vmem_limit_bytes=VMEM_LIMIT_BYTES)


def _const_spec(shape):
    nd = len(shape)
    return pl.BlockSpec(shape, lambda *_: (0,) * nd, pipeline_mode=pl.Buffered(1))


def _dot(a, b):
    return jnp.dot(a, b, preferred_element_type=f32)


def _dot_nt(a, b):
    return lax.dot_general(a, b, (((1,), (1,)), ((), ())), preferred_element_type=f32)


def _dot_tn(a, b):
    return lax.dot_general(a, b, (((0,), (0,)), ((), ())), preferred_element_type=f32)


def _dot_exact(a, b):
    return jnp.dot(a, b, preferred_element_type=f32, precision=lax.Precision.HIGHEST)


def _split_dot(x, m):
    hi = x.astype(bf16)
    lo = (x - hi.astype(f32)).astype(bf16)
    return _dot(hi, m) + _dot(lo, m)


def _modulated_norm(x, nw, scale, shift):
    ms = jnp.mean(x * x, axis=-1, keepdims=True)
    return (x * lax.rsqrt(ms + EPS) * nw) * (1.0 + scale) + shift


def _ada_kernel(c_ref, w_ref, b_ref, o_ref):
    c = c_ref[...]
    s = c * jax.nn.sigmoid(c)
    o_ref[0] = _dot(s.astype(bf16), w_ref[0].astype(bf16)) + b_ref[0]


def _ada_table(c_all, ada_w, ada_b):
    rows = c_all.shape[0]
    tn = 1536
    return pl.pallas_call(
        _ada_kernel,
        grid=(DEPTH, 6 * D_MODEL // tn),
        in_specs=[
            pl.BlockSpec((rows, D_MODEL), lambda l, j: (0, 0)),
            pl.BlockSpec((1, D_MODEL, tn), lambda l, j: (l, 0, j)),
            pl.BlockSpec((1, 1, tn), lambda l, j: (l, 0, j)),
        ],
        out_specs=pl.BlockSpec((1, rows, tn), lambda l, j: (l, 0, j)),
        out_shape=jax.ShapeDtypeStruct((DEPTH, rows, 6 * D_MODEL), f32),
        compiler_params=_cparams("parallel", "parallel"),
        name="ada_table",
    )(c_all, ada_w, ada_b.reshape(DEPTH, 1, 6 * D_MODEL))


def _premix_kernel(x_ref, ada_ref, nw_ref, w_ref, qw_ref, kw_ref, seg_ref,
                   hy_ref, q_ref, k_ref, v_ref, ret_ref, ssm_ref, dt_ref):
    x = x_ref[0]
    ada = ada_ref[0]
    h = _modulated_norm(x, nw_ref[...], ada[:, D_MODEL:2 * D_MODEL], ada[:, 0:D_MODEL])
    proj = _dot(h.astype(bf16), w_ref[...])
    hy_ref[0] = proj[:, 0:768]
    ret_ref[0] = proj[:, 1536:2304]
    ssm_ref[0] = proj[:, 2304:3072]
    dt_ref[0] = proj[:, 3072:3200]
    seg = seg_ref[...]

    def headnorm(t, w):
        return t * lax.rsqrt(_split_dot(t * t, seg) + EPS) * w

    qn = headnorm(proj[:, 768:1024], qw_ref[...]).astype(bf16)
    kn = headnorm(proj[:, 1024:1280], kw_ref[...]).astype(bf16)
    vv = proj[:, 1280:1536].astype(bf16)
    hw = 2 * DA_HD
    for hd in range(DA_HEADS):
        q_ref[0, hd] = qn[:, hd * hw:(hd + 1) * hw]
        k_ref[0, hd] = kn[:, hd * hw:(hd + 1) * hw]
        v_ref[0, hd] = vv[:, hd * hw:(hd + 1) * hw]


def _premix(x, ada, nw, w_in_pad, qw, kw, seg32, tt):
    B, L, _ = x.shape
    tok = lambda w: pl.BlockSpec((1, tt, w), lambda b, i: (b, i, 0))
    head = pl.BlockSpec((1, DA_HEADS, tt, 2 * DA_HD), lambda b, i: (b, 0, i, 0))
    head_shape = jax.ShapeDtypeStruct((B, DA_HEADS, L, 2 * DA_HD), bf16)
    return pl.pallas_call(
        _premix_kernel,
        grid=(B, L // tt),
        in_specs=[
            tok(D_MODEL),
            pl.BlockSpec((1, 1, 6 * D_MODEL), lambda b, i: (b, 0, 0)),
            _const_spec((1, D_MODEL)),
            _const_spec((D_MODEL, IN_COLS_PAD)),
            _const_spec((1, BR_W)),
            _const_spec((1, BR_W)),
            _const_spec((BR_W, BR_W)),
        ],
        out_specs=[tok(768), head, head, head, tok(768), tok(768), tok(LANES)],
        out_shape=[
            jax.ShapeDtypeStruct((B, L, 768), f32), head_shape, head_shape, head_shape,
            jax.ShapeDtypeStruct((B, L, 768), f32), jax.ShapeDtypeStruct((B, L, 768), f32),
            jax.ShapeDtypeStruct((B, L, LANES), f32),
        ],
        compiler_params=_cparams("parallel", "parallel"),
        name="premix",
    )(x, ada, nw, w_in_pad, qw, kw, seg32)


def _merge_kernel(x_ref, ada_ref, nw_ref, yhy_ref, yda_ref, yret_ref, yssm_ref,
                  wb_ref, wg_ref, bg_ref, wo_ref, o_ref):
    x = x_ref[0]
    ada = ada_ref[0]
    h = _modulated_norm(x, nw_ref[...], ada[:, D_MODEL:2 * D_MODEL], ada[:, 0:D_MODEL]).astype(bf16)
    merged = None
    for i in range(N_BRANCH):
        cols = slice(i * D_MODEL, (i + 1) * D_MODEL)
        gate = jax.nn.sigmoid(_dot(h, wg_ref[:, cols]) + bg_ref[:, cols])
        if i == 1:
            hw = 2 * DA_HD
            br = None
            for hd in range(DA_HEADS):
                t = _dot(yda_ref[0, hd].astype(bf16), wb_ref[1, hd * hw:(hd + 1) * hw, :])
                br = t if br is None else br + t
        else:
            y_ref = (yhy_ref, None, yret_ref, yssm_ref)[i]
            br = _dot(y_ref[0].astype(bf16), wb_ref[i])
        merged = gate * br if merged is None else merged + gate * br
    g1 = ada[:, 2 * D_MODEL:3 * D_MODEL]
    o_ref[0] = x + g1 * _dot(merged.astype(bf16), wo_ref[...])


def _merge(x, ada, nw, y_hy, y_da, y_ret, y_ssm, wb, wg, bg, wo, tt):
    B, L, _ = x.shape
    tok = lambda w: pl.BlockSpec((1, tt, w), lambda b, i: (b, i, 0))
    return pl.pallas_call(
        _merge_kernel,
        grid=(B, L // tt),
        in_specs=[
            tok(D_MODEL),
            pl.BlockSpec((1, 1, 6 * D_MODEL), lambda b, i: (b, 0, 0)),
            _const_spec((1, D_MODEL)),
            tok(BR_W),
            pl.BlockSpec((1, DA_HEADS, tt, 2 * DA_HD), lambda b, i: (b, 0, i, 0)),
            tok(BR_W), tok(BR_W),
            _const_spec((N_BRANCH, BR_W, D_MODEL)),
            _const_spec((D_MODEL, N_BRANCH * D_MODEL)),
            _const_spec((1, N_BRANCH * D_MODEL)),
            _const_spec((D_MODEL, D_MODEL)),
        ],
        out_specs=tok(D_MODEL),
        out_shape=jax.ShapeDtypeStruct((B, L, D_MODEL), f32),
        compiler_params=_cparams("parallel", "parallel"),
        name="merge",
    )(x, ada, nw, y_hy, y_da, y_ret, y_ssm, wb, wg, bg, wo)


def _top_values(s, n):
    rows = []
    for _ in range(n):
        m = jnp.max(s, axis=0, keepdims=True)
        rows.append(m)
        s = jnp.where(s == m, -jnp.inf, s)
    return jnp.concatenate(rows, axis=0)


def _peer_kernel(x_ref, ada_ref, nw_ref, wq_ref, bq_ref, keys_ref, u_ref, vt_ref, o_ref,
                 ht_ref, qt_ref, s_ref, top_ref, r_ref, e1_ref, e2_ref, acc_ref, *, tt, et):
    j = pl.program_id(2)
    ngroup = 2 * PEER_HEADS
    nlg = tt // LANES

    @pl.when(j == 0)
    def _():
        x = x_ref[0]
        ada = ada_ref[0]
        h2 = _modulated_norm(x, nw_ref[...], ada[:, 4 * D_MODEL:5 * D_MODEL], ada[:, 3 * D_MODEL:4 * D_MODEL])
        ht_ref[...] = h2.T.astype(bf16)
        bq = jnp.concatenate([bq_ref[...]] * nlg, axis=1)
        qt_ref[...] = (_dot(wq_ref[...], ht_ref[...]) + bq).astype(bf16)

        def score_body(g, c):
            row = pl.multiple_of(g * PEER_DK, PEER_DK)
            s_ref[g] = _dot(keys_ref[g], qt_ref[pl.ds(row, PEER_DK), :])
            for lg in range(nlg):
                cols = slice(lg * LANES, (lg + 1) * LANES)
                top_ref[g, :, cols] = _top_values(s_ref[g, :, cols], PEER_TOPK)
            return c

        lax.fori_loop(0, ngroup, score_body, 0)

        def head_body(hd, c):
            for lg in range(nlg):
                cols = slice(lg * LANES, (lg + 1) * LANES)
                v1 = top_ref[2 * hd, :, cols]
                v2 = top_ref[2 * hd + 1, :, cols]
                cand = [v1[0:1] + v2]
                cand += [v1[i:i + 1] + v2[0:8] for i in range(1, 8)]
                cand += [v1[8:16] + v2[0:1]]
                t = _top_values(jnp.concatenate(cand, axis=0), PEER_TOPK)
                z = jnp.sum(jnp.exp(t - t[0:1]), axis=0, keepdims=True)
                s1 = s_ref[2 * hd, :, cols]
                s2 = s_ref[2 * hd + 1, :, cols]
                r_ref[hd, :, cols] = t[PEER_TOPK - 1:PEER_TOPK] - s1
                e1_ref[hd, :, cols] = jnp.exp(s1 - v1[0:1]) / z
                e2_ref[hd, :, cols] = jnp.exp(s2 - v2[0:1])
            return c

        lax.fori_loop(0, PEER_HEADS, head_body, 0)
        acc_ref[...] = jnp.zeros_like(acc_ref)

    pre = _dot(u_ref[...], ht_ref[...])
    act = 0.5 * pre * (1.0 + lax.erf(pre * math.sqrt(0.5)))
    rows_per_tile = et // PEER_NK
    a0 = pl.multiple_of(j * rows_per_tile, SUBLANES)
    parts = []
    for i in range(rows_per_tile):
        lane_parts = []
        for lg in range(nlg):
            cols = slice(lg * LANES, (lg + 1) * LANES)
            w = None
            for hd in range(PEER_HEADS):
                r = r_ref[hd, pl.ds(a0, rows_per_tile), cols][i:i + 1]
                c = e1_ref[hd, pl.ds(a0, rows_per_tile), cols][i:i + 1]
                t = jnp.where(s_ref[2 * hd + 1, :, cols] >= r, e2_ref[hd, :, cols] * c, 0.0)
                w = t if w is None else w + t
            lane_parts.append(w * act[i * PEER_NK:(i + 1) * PEER_NK, cols])
        parts.append(jnp.concatenate(lane_parts, axis=1))
    p = jnp.concatenate(parts, axis=0).astype(bf16)
    acc_ref[...] += _dot(vt_ref[...], p)

    @pl.when(j == pl.num_programs(2) - 1)
    def _():
        g2 = ada_ref[0][:, 5 * D_MODEL:6 * D_MODEL]
        o_ref[0] = x_ref[0] + g2 * acc_ref[...].T


def _peer(x, ada, nw, wq_t, bq_rep, keys, u, v_t, tt, et):
    B, L, _ = x.shape
    nq = 2 * PEER_HEADS * PEER_DK
    kern = functools.partial(_peer_kernel, tt=tt, et=et)
    return pl.pallas_call(
        kern,
        grid=(B, L // tt, PEER_N // et),
        in_specs=[
            pl.BlockSpec((1, tt, D_MODEL), lambda b, i, j: (b, i, 0)),
            pl.BlockSpec((1, 1, 6 * D_MODEL), lambda b, i, j: (b, 0, 0)),
            _const_spec((1, D_MODEL)),
            _const_spec((nq, D_MODEL)),
            _const_spec((nq, LANES)),
            _const_spec((2 * PEER_HEADS, PEER_NK, PEER_DK)),
            pl.BlockSpec((et, D_MODEL), lambda b, i, j: (j, 0)),
            pl.BlockSpec((D_MODEL, et), lambda b, i, j: (0, j)),
        ],
        out_specs=pl.BlockSpec((1, tt, D_MODEL), lambda b, i, j: (b, i, 0)),
        out_shape=jax.ShapeDtypeStruct((B, L, D_MODEL), f32),
        scratch_shapes=[
            pltpu.VMEM((D_MODEL, tt), bf16),
            pltpu.VMEM((nq, tt), bf16),
            pltpu.VMEM((2 * PEER_HEADS, PEER_NK, tt), f32),
            pltpu.VMEM((2 * PEER_HEADS, PEER_TOPK, tt), f32),
            pltpu.VMEM((PEER_HEADS, PEER_NK, tt), f32),
            pltpu.VMEM((PEER_HEADS, PEER_NK, tt), f32),
            pltpu.VMEM((PEER_HEADS, PEER_NK, tt), f32),
            pltpu.VMEM((D_MODEL, tt), f32),
        ],
        compiler_params=_cparams("parallel", "parallel", "arbitrary"),
        name="peer",
    )(x, ada, nw, wq_t, bq_rep, keys, u, v_t)


def _attn_kernel(q_ref, k_ref, v_ref, lq1_ref, lk1_ref, lq2_ref, lk2_ref, sw_ref, o_ref, *, tq, tk, nk, lam_init):
    hd = pl.program_id(1)
    qi = pl.program_id(2)
    q = q_ref[0, 0]
    lane = lax.broadcasted_iota(jnp.int32, (1, 2 * DA_HD), 1)
    zero = jnp.zeros_like(q)
    q_maps = (jnp.where(lane < DA_HD, q, zero), jnp.where(lane >= DA_HD, q, zero))
    slope = jnp.exp2(-(8.0 / DA_HEADS) * (jnp.full((1, 1), hd, jnp.int32).astype(f32) + 1.0))
    rel = (lax.broadcasted_iota(jnp.int32, (tq, tk), 0) - lax.broadcasted_iota(jnp.int32, (tq, tk), 1)
           + jnp.full((1, 1), qi * tq, jnp.int32)).astype(f32)

    def body(kb, carry):
        start = pl.multiple_of(kb * tk, tk)
        k = k_ref[0, 0, pl.ds(start, tk), :]
        v = v_ref[0, 0, pl.ds(start, tk), :]
        bias = -slope * jnp.abs(rel - jnp.full((1, 1), kb * tk, jnp.int32).astype(f32))
        out = []
        for c in range(2):
            m, l, a = carry[3 * c:3 * c + 3]
            s = _dot_nt(q_maps[c], k) + bias
            mn = jnp.maximum(m, jnp.max(s, axis=-1, keepdims=True))
            p = jnp.exp(s - mn)
            alpha = jnp.exp(m - mn)
            out += [mn, alpha * l + jnp.sum(p, axis=-1, keepdims=True), alpha * a + _dot(p.astype(bf16), v)]
        return tuple(out)

    one = (jnp.full((tq, 1), -jnp.inf, f32), jnp.zeros((tq, 1), f32), jnp.zeros((tq, 2 * DA_HD), f32))
    _, l0, a0, _, l1, a1 = lax.fori_loop(0, nk, body, one + one)
    lam = (jnp.exp(jnp.sum(lq1_ref[...] * lk1_ref[...], axis=-1, keepdims=True))
           - jnp.exp(jnp.sum(lq2_ref[...] * lk2_ref[...], axis=-1, keepdims=True)) + lam_init)
    o = a0 / l0 - lam * (a1 / l1)
    ms = jnp.mean(o * o, axis=-1, keepdims=True)
    o_ref[0, 0] = (o * lax.rsqrt(ms + EPS) * sw_ref[0]) * (1.0 - lam_init)


def _attention(q, k, v, lq1, lk1, lq2, lk2, subw, lam_init, tq, tk):
    B, H, L, hw = q.shape
    kern = functools.partial(_attn_kernel, tq=tq, tk=tk, nk=L // tk, lam_init=lam_init)
    row = lambda: _const_spec((1, DA_HD))
    return pl.pallas_call(
        kern,
        grid=(B, H, L // tq),
        in_specs=[
            pl.BlockSpec((1, 1, tq, hw), lambda b, h, i: (b, h, i, 0)),
            pl.BlockSpec((1, 1, L, hw), lambda b, h, i: (b, h, 0, 0)),
            pl.BlockSpec((1, 1, L, hw), lambda b, h, i: (b, h, 0, 0)),
            row(), row(), row(), row(),
            pl.BlockSpec((1, 1, hw), lambda b, h, i: (h, 0, 0)),
        ],
        out_specs=pl.BlockSpec((1, 1, tq, hw), lambda b, h, i: (b, h, i, 0)),
        out_shape=jax.ShapeDtypeStruct((B, H, L, hw), f32),
        compiler_params=_cparams("parallel", "parallel", "parallel"),
        name="diff_attention",
    )(q, k, v, lq1, lk1, lq2, lk2, subw)


def _rmsnorm(x, w):
    return x * lax.rsqrt(jnp.mean(x * x, axis=-1, keepdims=True) + EPS) * w


def _dwconv(x, w, b):
    K, C = w.shape
    y = lax.conv_general_dilated(x, w[:, None, :], window_strides=(1,), padding=[(K // 2, K // 2)],
                                 dimension_numbers=('NWC', 'WIO', 'NWC'), feature_group_count=C)
    return y + b


def _chunk_scan(q, k, v, log_a):
    Bsz, L, H, N = q.shape
    P = v.shape[-1]
    nc = L // CHUNK
    qc = q.reshape(Bsz, nc, CHUNK, H, N)
    kc = k.reshape(Bsz, nc, CHUNK, H, N)
    vc = v.reshape(Bsz, nc, CHUNK, H, P)
    cum = jnp.cumsum(log_a.reshape(Bsz, nc, CHUNK, H), axis=2)
    cum_h = jnp.moveaxis(cum, 3, 2)
    causal = jnp.tril(jnp.ones((CHUNK, CHUNK), dtype=bool))
    seg = jnp.where(causal, cum_h[..., :, None] - cum_h[..., None, :], -jnp.inf)
    scores = jnp.einsum('bcihn,bcjhn->bchij', qc, kc) * jnp.exp(seg)
    y = jnp.einsum('bchij,bcjhp->bcihp', scores, vc)
    to_end = jnp.exp(cum[:, :, -1:, :] - cum)
    chunk_states = jnp.einsum('bcjhn,bcjh,bcjhp->bchnp', kc, to_end, vc)
    chunk_decay = jnp.exp(cum[:, :, -1, :])

    def step(h, inp):
        s, d = inp
        return h * d[..., None, None] + s, h

    h0 = jnp.zeros((Bsz, H, N, P), chunk_states.dtype)
    _, h_prev = lax.scan(step, h0, (jnp.moveaxis(chunk_states, 1, 0), jnp.moveaxis(chunk_decay, 1, 0)))
    h_prev = jnp.moveaxis(h_prev, 0, 1)
    y = y + jnp.einsum('bcihn,bcih,bchnp->bcihp', qc, jnp.exp(cum), h_prev)
    return y.reshape(Bsz, L, H, P)


def _bidir_scan(q, k_fwd, k_bwd, v, la_fwd, la_bwd):
    fl = lambda a: jnp.flip(a, axis=1)
    return _chunk_scan(q, k_fwd, v, la_fwd) + fl(_chunk_scan(fl(q), fl(k_bwd), fl(v), fl(la_bwd)))


def _hyena_filters(L, w1, b1, w2, b2, w3, freq):
    pos = jnp.arange(L, dtype=f32)
    t = pos / (L - 1)
    bands = jnp.linspace(1e-4, HY_BANDS - 1, HY_BANDS, dtype=f32)
    ang = (2.0 * math.pi / L) * pos[:, None] * bands[None, :]
    emb = jnp.concatenate([t[:, None], jnp.cos(ang), -jnp.sin(ang)], axis=-1)
    hid = jnp.sin(freq * (emb @ w1 + b1))
    hid = jnp.sin(freq * (hid @ w2 + b2))
    filt = hid @ w3
    deltas = jnp.abs(jnp.linspace(HY_MIN_DECAY, HY_MAX_DECAY, BR_W, dtype=f32))
    window = jnp.exp(-t[:, None] * jnp.tile(deltas, HY_ORDER * 2)[None, :])
    filt = filt * window
    filt = filt * lax.rsqrt(jnp.sum(filt * filt, axis=0, keepdims=True) + 1e-6)
    return filt.reshape(L, HY_ORDER, 2, BR_W)


def _bidir_fftconv(z, h_fwd, h_bwd, skip):
    L, C = h_fwd.shape
    kern = jnp.concatenate([h_fwd, jnp.zeros((1, C), h_fwd.dtype), h_bwd[:0:-1]], axis=0)
    kf = jnp.fft.rfft(kern, n=2 * L, axis=0)
    zf = jnp.fft.rfft(z, n=2 * L, axis=1)
    y = jnp.fft.irfft(zf * kf[None], n=2 * L, axis=1)[:, :L]
    return y + z * skip


def _hyena_branch(p, W, l):
    L = p.shape[1]
    u = _dwconv(p, W['hy_conv_w'][l], W['hy_conv_b'][l])
    v, x1, x2 = jnp.split(u, 3, axis=-1)
    filt = _hyena_filters(L, W['hf_w1'][l], W['hf_b1'][l], W['hf_w2'][l], W['hf_b2'][l], W['hf_w3'][l], W['hf_freq'][l])
    z = v
    for o, gate in enumerate((x1, x2)):
        z = gate * _bidir_fftconv(z, filt[:, o, 0], filt[:, o, 1], W['hy_bias'][l, o])
    return z


def _retention_branch(p, W, l):
    B, L, _ = p.shape
    qk = RET_HEADS * RET_DK
    q, k, v, g = jnp.split(p, [qk, 2 * qk, 2 * qk + BR_W], axis=-1)
    q = q.reshape(B, L, RET_HEADS, RET_DK)
    k = k.reshape(B, L, RET_HEADS, RET_DK) * (RET_DK ** -0.5)
    v = v.reshape(B, L, RET_HEADS, RET_DV)
    hidx = jnp.arange(RET_HEADS, dtype=f32)
    la_f = jnp.broadcast_to(jnp.log1p(-(2.0 ** (-5.0 - hidx))), (B, L, RET_HEADS))
    la_b = jnp.broadcast_to(jnp.log1p(-(2.0 ** (-5.5 - hidx))), (B, L, RET_HEADS))
    y = _bidir_scan(q, k, k, v, la_f, la_b)
    y = _rmsnorm(y, W['ret_gn_w'][l].reshape(RET_HEADS, RET_DV)).reshape(B, L, BR_W)
    return jax.nn.silu(g) * y


def _ssd_branch(p, dt, W, l):
    B, L, _ = p.shape
    z, xbc = jnp.split(p, [BR_W], axis=-1)
    xbc = jax.nn.silu(_dwconv(xbc, W['ssm_conv_w'][l], W['ssm_conv_b'][l]))
    xs, Bm, Cm = jnp.split(xbc, [BR_W, BR_W + SSM_GROUPS * SSM_N], axis=-1)
    xs = xs.reshape(B, L, SSM_HEADS, SSM_P)
    rep = SSM_HEADS // SSM_GROUPS
    Bh = jnp.repeat(Bm.reshape(B, L, SSM_GROUPS, SSM_N), rep, axis=2)
    Ch = jnp.repeat(Cm.reshape(B, L, SSM_GROUPS, SSM_N), rep, axis=2)
    dt = jax.nn.softplus(dt.reshape(B, L, 2, SSM_HEADS) + W['ssm_dt_bias'][l])
    A = -jnp.exp(W['ssm_A_log'][l])
    y = _bidir_scan(Ch, Bh * dt[:, :, 0, :, None], Bh * dt[:, :, 1, :, None], xs,
                    dt[:, :, 0] * A[0], dt[:, :, 1] * A[1])
    y = y + W['ssm_D'][l][:, None] * xs
    y = y.reshape(B, L, BR_W) * jax.nn.silu(z)
    return _rmsnorm(y, W['ssm_norm_w'][l])


TOKEN_TILE = 512
PEER_TOKEN_TILE = 512
PEER_EXPERT_TILE = SUBLANES * PEER_NK
ATTN_Q_TILE = 256
ATTN_K_TILE = 512


def _layer_weights(W, l):
    seg = np.kron(np.eye(BR_W // DA_HD), np.ones((DA_HD, DA_HD))) / DA_HD
    nq = 2 * PEER_HEADS * PEER_DK
    return dict(
        nw1=W['norm1_w'][l].reshape(1, D_MODEL),
        nw2=W['norm2_w'][l].reshape(1, D_MODEL),
        w_in=jnp.pad(W['w_in'][l], ((0, 0), (0, IN_COLS_PAD - W['w_in'].shape[-1]))).astype(bf16),
        qw=(jnp.tile(W['da_qnorm_w'][l], BR_W // DA_HD) * (DA_HD ** -0.5)).reshape(1, BR_W),
        kw=jnp.tile(W['da_knorm_w'][l], BR_W // DA_HD).reshape(1, BR_W),
        seg=jnp.asarray(seg, bf16),
        lam=[W[n][l].reshape(1, DA_HD) for n in ('da_lq1', 'da_lk1', 'da_lq2', 'da_lk2')],
        subw=W['da_subln_w'][l].reshape(DA_HEADS, 1, 2 * DA_HD),
        wb=W['w_branch'][l].astype(bf16),
        wg=jnp.transpose(W['w_gate'][l], (1, 0, 2)).reshape(D_MODEL, N_BRANCH * D_MODEL).astype(bf16),
        bg=W['b_gate'][l].reshape(1, N_BRANCH * D_MODEL),
        wo=W['w_out'][l].astype(bf16),
        wq_t=W['peer_wq'][l].T.astype(bf16),
        bq=jnp.broadcast_to(W['peer_bq'][l][:, None], (nq, LANES)),
        keys=W['peer_keys'][l].reshape(2 * PEER_HEADS, PEER_NK, PEER_DK).astype(bf16),
        u=W['peer_u'][l].astype(bf16),
        v_t=W['peer_v'][l].T.astype(bf16),
    )


def _layer(x, ada, W, lw, l):
    p_hy, q, k, v, p_ret, p_ssm, p_dt = _premix(x, ada, lw['nw1'], lw['w_in'], lw['qw'], lw['kw'], lw['seg'], TOKEN_TILE)
    lam_init = 0.8 - 0.6 * math.exp(-0.3 * l)
    L = x.shape[1]
    y_da = _attention(q, k, v, *lw['lam'], lw['subw'], lam_init, min(ATTN_Q_TILE, L), min(ATTN_K_TILE, L))
    y_hy = _hyena_branch(p_hy, W, l)
    y_ret = _retention_branch(p_ret, W, l)
    y_ssm = _ssd_branch(p_ssm, p_dt[..., :2 * SSM_HEADS], W, l)
    x = _merge(x, ada, lw['nw1'], y_hy, y_da, y_ret, y_ssm, lw['wb'], lw['wg'], lw['bg'], lw['wo'], TOKEN_TILE)
    return _peer(x, ada, lw['nw2'], lw['wq_t'], lw['bq'], lw['keys'], lw['u'], lw['v_t'],
                 PEER_TOKEN_TILE, PEER_EXPERT_TILE)


def kernel(x_prompt, x_sample, c_prompt, c_sample, ada_w, ada_b, norm1_w, norm2_w, w_in, hy_conv_w, hy_conv_b,
           hf_w1, hf_b1, hf_w2, hf_b2, hf_w3, hf_freq, hy_bias, da_qnorm_w, da_knorm_w, da_lq1, da_lk1, da_lq2,
           da_lk2, da_subln_w, ret_gn_w, ssm_conv_w, ssm_conv_b, ssm_dt_bias, ssm_A_log, ssm_D, ssm_norm_w,
           w_branch, w_gate, b_gate, w_out, peer_wq, peer_bq, peer_keys, peer_u, peer_v):
    W = dict(norm1_w=norm1_w, norm2_w=norm2_w, w_in=w_in, hy_conv_w=hy_conv_w, hy_conv_b=hy_conv_b,
             hf_w1=hf_w1, hf_b1=hf_b1, hf_w2=hf_w2, hf_b2=hf_b2, hf_w3=hf_w3, hf_freq=hf_freq, hy_bias=hy_bias,
             da_qnorm_w=da_qnorm_w, da_knorm_w=da_knorm_w, da_lq1=da_lq1, da_lk1=da_lk1, da_lq2=da_lq2,
             da_lk2=da_lk2, da_subln_w=da_subln_w, ret_gn_w=ret_gn_w, ssm_conv_w=ssm_conv_w,
             ssm_conv_b=ssm_conv_b, ssm_dt_bias=ssm_dt_bias, ssm_A_log=ssm_A_log, ssm_D=ssm_D,
             ssm_norm_w=ssm_norm_w, w_branch=w_branch, w_gate=w_gate, b_gate=b_gate, w_out=w_out,
             peer_wq=peer_wq, peer_bq=peer_bq, peer_keys=peer_keys, peer_u=peer_u, peer_v=peer_v)
    nb = x_prompt.shape[0]
    ada_all = _ada_table(jnp.concatenate([c_prompt, c_sample], axis=0), ada_w, ada_b)
    xs = [x_prompt, x_sample]
    for l in range(DEPTH):
        lw = _layer_weights(W, l)
        adas = (ada_all[l, :nb][:, None, :], ada_all[l, nb:][:, None, :])
        xs = [_layer(x, ada, W, lw, l) for x, ada in zip(xs, adas)]
    return tuple(xs)
```

```python
import functools
import math

import jax
import jax.numpy as jnp
import numpy as np
from jax import lax
from jax.experimental import pallas as pl
from jax.experimental.pallas import tpu as pltpu

f32 = jnp.float32
bf16 = jnp.bfloat16

LANES = 128
SUBLANES = 8
VMEM_LIMIT_BYTES = 48 * 1024 * 1024

D_MODEL = 1024
DEPTH = 4
N_BRANCH = 4
BR_W = D_MODEL // N_BRANCH
HY_BANDS = 16
HY_EMB = 1 + 2 * HY_BANDS
HY_HID = 64
HY_ORDER = 2
HY_MIN_DECAY = math.log(1e-2) / 1.5
HY_MAX_DECAY = math.log(1e-2) / 0.3
DA_HEADS = 4
DA_HD = BR_W // (2 * DA_HEADS)
RET_HEADS = 4
RET_DK = BR_W // (2 * RET_HEADS)
RET_DV = BR_W // RET_HEADS
SSM_HEADS = 4
SSM_P = BR_W // SSM_HEADS
SSM_GROUPS = 2
SSM_N = 64
SSM_XBC = BR_W + 2 * SSM_GROUPS * SSM_N
CHUNK = 128
PEER_HEADS = 8
PEER_NK = 128
PEER_N = PEER_NK * PEER_NK
PEER_DK = 128
PEER_TOPK = 16
IN_COLS_PAD = 3 * 768 + 768 + LANES
EPS = 1e-6


def _cparams(*sem):
    return pltpu.CompilerParams(dimension_semantics=sem, vmem_limit_bytes=VMEM_LIMIT_BYTES)


def _const_spec(shape):
    nd = len(shape)
    return pl.BlockSpec(shape, lambda *_: (0,) * nd, pipeline_mode=pl.Buffered(1))


def _dot(a, b):
    return jnp.dot(a, b, preferred_element_type=f32)


def _dot_nt(a, b):
    return lax.dot_general(a, b, (((1,), (1,)), ((), ())), preferred_element_type=f32)


def _dot_tn(a, b):
    return lax.dot_general(a, b, (((0,), (0,)), ((), ())), preferred_element_type=f32)


def _dot_exact(a, b):
    return jnp.dot(a, b, preferred_element_type=f32, precision=lax.Precision.HIGHEST)


def _split_dot(x, m):
    hi = x.astype(bf16)
    lo = (x - hi.astype(f32)).astype(bf16)
    return _dot(hi, m) + _dot(lo, m)


def _modulated_norm(x, nw, scale, shift):
    ms = jnp.mean(x * x, axis=-1, keepdims=True)
    return (x * lax.rsqrt(ms + EPS) * nw) * (1.0 + scale) + shift


def _ada_kernel(c_ref, w_ref, b_ref, o_ref):
    c = c_ref[...]
    s = c * jax.nn.sigmoid(c)
    o_ref[0] = _dot(s.astype(bf16), w_ref[0].astype(bf16)) + b_ref[0]


def _ada_table(c_all, ada_w, ada_b):
    rows = c_all.shape[0]
    tn = 1536
    return pl.pallas_call(
        _ada_kernel,
        grid=(DEPTH, 6 * D_MODEL // tn),
        in_specs=[
            pl.BlockSpec((rows, D_MODEL), lambda l, j: (0, 0)),
            pl.BlockSpec((1, D_MODEL, tn), lambda l, j: (l, 0, j)),
            pl.BlockSpec((1, 1, tn), lambda l, j: (l, 0, j)),
        ],
        out_specs=pl.BlockSpec((1, rows, tn), lambda l, j: (l, 0, j)),
        out_shape=jax.ShapeDtypeStruct((DEPTH, rows, 6 * D_MODEL), f32),
        compiler_params=_cparams("parallel", "parallel"),
        name="ada_table",
    )(c_all, ada_w, ada_b.reshape(DEPTH, 1, 6 * D_MODEL))


def _premix_kernel(x_ref, ada_ref, nw_ref, w_ref, qw_ref, kw_ref, seg_ref,
                   hy_ref, q_ref, k_ref, v_ref, ret_ref, ssm_ref, dt_ref):
    x = x_ref[0]
    ada = ada_ref[0]
    h = _modulated_norm(x, nw_ref[...], ada[:, D_MODEL:2 * D_MODEL], ada[:, 0:D_MODEL])
    proj = _dot(h.astype(bf16), w_ref[...])
    hy_ref[0] = proj[:, 0:768]
    ret_ref[0] = proj[:, 1536:2304]
    ssm_ref[0] = proj[:, 2304:3072]
    dt_ref[0] = proj[:, 3072:3200]
    seg = seg_ref[...]

    def headnorm(t, w):
        return t * lax.rsqrt(_split_dot(t * t, seg) + EPS) * w

    qn = headnorm(proj[:, 768:1024], qw_ref[...]).astype(bf16)
    kn = headnorm(proj[:, 1024:1280], kw_ref[...]).astype(bf16)
    vv = proj[:, 1280:1536].astype(bf16)
    hw = 2 * DA_HD
    for hd in range(DA_HEADS):
        q_ref[0, hd] = qn[:, hd * hw:(hd + 1) * hw]
        k_ref[0, hd] = kn[:, hd * hw:(hd + 1) * hw]
        v_ref[0, hd] = vv[:, hd * hw:(hd + 1) * hw]


def _premix(x, ada, nw, w_in_pad, qw, kw, seg32, tt):
    B, L, _ = x.shape
    tok = lambda w: pl.BlockSpec((1, tt, w), lambda b, i: (b, i, 0))
    head = pl.BlockSpec((1, DA_HEADS, tt, 2 * DA_HD), lambda b, i: (b, 0, i, 0))
    head_shape = jax.ShapeDtypeStruct((B, DA_HEADS, L, 2 * DA_HD), bf16)
    return pl.pallas_call(
        _premix_kernel,
        grid=(B, L // tt),
        in_specs=[
            tok(D_MODEL),
            pl.BlockSpec((1, 1, 6 * D_MODEL), lambda b, i: (b, 0, 0)),
            _const_spec((1, D_MODEL)),
            _const_spec((D_MODEL, IN_COLS_PAD)),
            _const_spec((1, BR_W)),
            _const_spec((1, BR_W)),
            _const_spec((BR_W, BR_W)),
        ],
        out_specs=[tok(768), head, head, head, tok(768), tok(768), tok(LANES)],
        out_shape=[
            jax.ShapeDtypeStruct((B, L, 768), f32), head_shape, head_shape, head_shape,
            jax.ShapeDtypeStruct((B, L, 768), f32), jax.ShapeDtypeStruct((B, L, 768), f32),
            jax.ShapeDtypeStruct((B, L, LANES), f32),
        ],
        compiler_params=_cparams("parallel", "parallel"),
        name="premix",
    )(x, ada, nw, w_in_pad, qw, kw, seg32)


def _merge_kernel(x_ref, ada_ref, nw_ref, yhy_ref, yda_ref, yret_ref, yssm_ref,
                  wb_ref, wg_ref, bg_ref, wo_ref, o_ref):
    x = x_ref[0]
    ada = ada_ref[0]
    h = _modulated_norm(x, nw_ref[...], ada[:, D_MODEL:2 * D_MODEL], ada[:, 0:D_MODEL]).astype(bf16)
    merged = None
    for i in range(N_BRANCH):
        cols = slice(i * D_MODEL, (i + 1) * D_MODEL)
        gate = jax.nn.sigmoid(_dot(h, wg_ref[:, cols]) + bg_ref[:, cols])
        if i == 1:
            hw = 2 * DA_HD
            br = None
            for hd in range(DA_HEADS):
                t = _dot(yda_ref[0, hd].astype(bf16), wb_ref[1, hd * hw:(hd + 1) * hw, :])
                br = t if br is None else br + t
        else:
            y_ref = (yhy_ref, None, yret_ref, yssm_ref)[i]
            br = _dot(y_ref[0].astype(bf16), wb_ref[i])
        merged = gate * br if merged is None else merged + gate * br
    g1 = ada[:, 2 * D_MODEL:3 * D_MODEL]
    o_ref[0] = x + g1 * _dot(merged.astype(bf16), wo_ref[...])


def _merge(x, ada, nw, y_hy, y_da, y_ret, y_ssm, wb, wg, bg, wo, tt):
    B, L, _ = x.shape
    tok = lambda w: pl.BlockSpec((1, tt, w), lambda b, i: (b, i, 0))
    return pl.pallas_call(
        _merge_kernel,
        grid=(B, L // tt),
        in_specs=[
            tok(D_MODEL),
            pl.BlockSpec((1, 1, 6 * D_MODEL), lambda b, i: (b, 0, 0)),
            _const_spec((1, D_MODEL)),
            tok(BR_W),
            pl.BlockSpec((1, DA_HEADS, tt, 2 * DA_HD), lambda b, i: (b, 0, i, 0)),
            tok(BR_W), tok(BR_W),
            _const_spec((N_BRANCH, BR_W, D_MODEL)),
            _const_spec((D_MODEL, N_BRANCH * D_MODEL)),
            _const_spec((1, N_BRANCH * D_MODEL)),
            _const_spec((D_MODEL, D_MODEL)),
        ],
        out_specs=tok(D_MODEL),
        out_shape=jax.ShapeDtypeStruct((B, L, D_MODEL), f32),
        compiler_params=_cparams("parallel", "parallel"),
        name="merge",
    )(x, ada, nw, y_hy, y_da, y_ret, y_ssm, wb, wg, bg, wo)


def _top_values(s, n):
    rows = []
    for _ in range(n):
        m = jnp.max(s, axis=0, keepdims=True)
        rows.append(m)
        s = jnp.where(s == m, -jnp.inf, s)
    return jnp.concatenate(rows, axis=0)


def _peer_kernel(x_ref, ada_ref, nw_ref, wq_ref, bq_ref, keys_ref, u_ref, vt_ref, o_ref,
                 ht_ref, qt_ref, s_ref, top_ref, r_ref, e1_ref, e2_ref, acc_ref, *, tt, et):
    j = pl.program_id(2)
    ngroup = 2 * PEER_HEADS
    nlg = tt // LANES

    @pl.when(j == 0)
    def _():
        x = x_ref[0]
        ada = ada_ref[0]
        h2 = _modulated_norm(x, nw_ref[...], ada[:, 4 * D_MODEL:5 * D_MODEL], ada[:, 3 * D_MODEL:4 * D_MODEL])
        ht_ref[...] = h2.T.astype(bf16)
        bq = jnp.concatenate([bq_ref[...]] * nlg, axis=1)
        qt_ref[...] = (_dot(wq_ref[...], ht_ref[...]) + bq).astype(bf16)

        def score_body(g, c):
            row = pl.multiple_of(g * PEER_DK, PEER_DK)
            s_ref[g] = _dot(keys_ref[g], qt_ref[pl.ds(row, PEER_DK), :])
            for lg in range(nlg):
                cols = slice(lg * LANES, (lg + 1) * LANES)
                top_ref[g, :, cols] = _top_values(s_ref[g, :, cols], PEER_TOPK)
            return c

        lax.fori_loop(0, ngroup, score_body, 0)

        def head_body(hd, c):
            for lg in range(nlg):
                cols = slice(lg * LANES, (lg + 1) * LANES)
                v1 = top_ref[2 * hd, :, cols]
                v2 = top_ref[2 * hd + 1, :, cols]
                cand = [v1[0:1] + v2]
                cand += [v1[i:i + 1] + v2[0:8] for i in range(1, 8)]
                cand += [v1[8:16] + v2[0:1]]
                t = _top_values(jnp.concatenate(cand, axis=0), PEER_TOPK)
                z = jnp.sum(jnp.exp(t - t[0:1]), axis=0, keepdims=True)
                s1 = s_ref[2 * hd, :, cols]
                s2 = s_ref[2 * hd + 1, :, cols]
                r_ref[hd, :, cols] = t[PEER_TOPK - 1:PEER_TOPK] - s1
                e1_ref[hd, :, cols] = jnp.exp(s1 - v1[0:1]) / z
                e2_ref[hd, :, cols] = jnp.exp(s2 - v2[0:1])
            return c

        lax.fori_loop(0, PEER_HEADS, head_body, 0)
        acc_ref[...] = jnp.zeros_like(acc_ref)

    rows_per_tile = et // PEER_NK
    a0 = pl.multiple_of(j * rows_per_tile, SUBLANES)
    rows_per_sub = 2
    total = None
    for sub in range(rows_per_tile // rows_per_sub):
        erows = slice(sub * rows_per_sub * PEER_NK, (sub + 1) * rows_per_sub * PEER_NK)
        pre = _dot(u_ref[erows, :], ht_ref[...])
        act = 0.5 * pre * (1.0 + lax.erf(pre * math.sqrt(0.5)))
        parts = []
        for k in range(rows_per_sub):
            i = sub * rows_per_sub + k
            lane_parts = []
            for lg in range(nlg):
                cols = slice(lg * LANES, (lg + 1) * LANES)
                w = None
                for hd in range(PEER_HEADS):
                    r = r_ref[hd, pl.ds(a0, rows_per_tile), cols][i:i + 1]
                    c = e1_ref[hd, pl.ds(a0, rows_per_tile), cols][i:i + 1]
                    t = jnp.where(s_ref[2 * hd + 1, :, cols] >= r, e2_ref[hd, :, cols] * c, 0.0)
                    w = t if w is None else w + t
                lane_parts.append(w * act[k * PEER_NK:(k + 1) * PEER_NK, cols])
            parts.append(jnp.concatenate(lane_parts, axis=1))
        p = jnp.concatenate(parts, axis=0).astype(bf16)
        t = _dot(vt_ref[:, erows], p)
        total = t if total is None else total + t
    acc_ref[...] += total

    @pl.when(j == pl.num_programs(2) - 1)
    def _():
        g2 = ada_ref[0][:, 5 * D_MODEL:6 * D_MODEL]
        o_ref[0] = x_ref[0] + g2 * acc_ref[...].T


def _peer(x, ada, nw, wq_t, bq_rep, keys, u, v_t, tt, et):
    B, L, _ = x.shape
    nq = 2 * PEER_HEADS * PEER_DK
    kern = functools.partial(_peer_kernel, tt=tt, et=et)
    return pl.pallas_call(
        kern,
        grid=(B, L // tt, PEER_N // et),
        in_specs=[
            pl.BlockSpec((1, tt, D_MODEL), lambda b, i, j: (b, i, 0)),
            pl.BlockSpec((1, 1, 6 * D_MODEL), lambda b, i, j: (b, 0, 0)),
            _const_spec((1, D_MODEL)),
            _const_spec((nq, D_MODEL)),
            _const_spec((nq, LANES)),
            _const_spec((2 * PEER_HEADS, PEER_NK, PEER_DK)),
            pl.BlockSpec((et, D_MODEL), lambda b, i, j: (j, 0)),
            pl.BlockSpec((D_MODEL, et), lambda b, i, j: (0, j)),
        ],
        out_specs=pl.BlockSpec((1, tt, D_MODEL), lambda b, i, j: (b, i, 0)),
        out_shape=jax.ShapeDtypeStruct((B, L, D_MODEL), f32),
        scratch_shapes=[
            pltpu.VMEM((D_MODEL, tt), bf16),
            pltpu.VMEM((nq, tt), bf16),
            pltpu.VMEM((2 * PEER_HEADS, PEER_NK, tt), f32),
            pltpu.VMEM((2 * PEER_HEADS, PEER_TOPK, tt), f32),
            pltpu.VMEM((PEER_HEADS, PEER_NK, tt), f32),
            pltpu.VMEM((PEER_HEADS, PEER_NK, tt), f32),
            pltpu.VMEM((PEER_HEADS, PEER_NK, tt), f32),
            pltpu.VMEM((D_MODEL, tt), f32),
        ],
        compiler_params=_cparams("parallel", "parallel", "arbitrary"),
        name="peer",
    )(x, ada, nw, wq_t, bq_rep, keys, u, v_t)


def _attn_kernel(q_ref, k_ref, v_ref, lq1_ref, lk1_ref, lq2_ref, lk2_ref, sw_ref, o_ref, *, tq, tk, nk, lam_init):
    hd = pl.program_id(1)
    qi = pl.program_id(2)
    q = q_ref[0, 0]
    lane = lax.broadcasted_iota(jnp.int32, (1, 2 * DA_HD), 1)
    zero = jnp.zeros_like(q)
    q_maps = (jnp.where(lane < DA_HD, q, zero), jnp.where(lane >= DA_HD, q, zero))
    slope = jnp.exp2(-(8.0 / DA_HEADS) * (jnp.full((1, 1), hd, jnp.int32).astype(f32) + 1.0))
    rel = (lax.broadcasted_iota(jnp.int32, (tq, tk), 0) - lax.broadcasted_iota(jnp.int32, (tq, tk), 1)
           + jnp.full((1, 1), qi * tq, jnp.int32)).astype(f32)

    def body(kb, carry):
        start = pl.multiple_of(kb * tk, tk)
        k = k_ref[0, 0, pl.ds(start, tk), :]
        v = v_ref[0, 0, pl.ds(start, tk), :]
        bias = -slope * jnp.abs(rel - jnp.full((1, 1), kb * tk, jnp.int32).astype(f32))
        out = []
        for c in range(2):
            m, l, a = carry[3 * c:3 * c + 3]
            s = _dot_nt(q_maps[c], k) + bias
            mn = jnp.maximum(m, jnp.max(s, axis=-1, keepdims=True))
            p = jnp.exp(s - mn)
            alpha = jnp.exp(m - mn)
            out += [mn, alpha * l + jnp.sum(p, axis=-1, keepdims=True), alpha * a + _dot(p.astype(bf16), v)]
        return tuple(out)

    one = (jnp.full((tq, 1), -jnp.inf, f32), jnp.zeros((tq, 1), f32), jnp.zeros((tq, 2 * DA_HD), f32))
    _, l0, a0, _, l1, a1 = lax.fori_loop(0, nk, body, one + one)
    lam = (jnp.exp(jnp.sum(lq1_ref[...] * lk1_ref[...], axis=-1, keepdims=True))
           - jnp.exp(jnp.sum(lq2_ref[...] * lk2_ref[...], axis=-1, keepdims=True)) + lam_init)
    o = a0 / l0 - lam * (a1 / l1)
    ms = jnp.mean(o * o, axis=-1, keepdims=True)
    o_ref[0, 0] = (o * lax.rsqrt(ms + EPS) * sw_ref[0]) * (1.0 - lam_init)


def _attention(q, k, v, lq1, lk1, lq2, lk2, subw, lam_init, tq, tk):
    B, H, L, hw = q.shape
    kern = functools.partial(_attn_kernel, tq=tq, tk=tk, nk=L // tk, lam_init=lam_init)
    row = lambda: _const_spec((1, DA_HD))
    return pl.pallas_call(
        kern,
        grid=(B, H, L // tq),
        in_specs=[
            pl.BlockSpec((1, 1, tq, hw), lambda b, h, i: (b, h, i, 0)),
            pl.BlockSpec((1, 1, L, hw), lambda b, h, i: (b, h, 0, 0)),
            pl.BlockSpec((1, 1, L, hw), lambda b, h, i: (b, h, 0, 0)),
            row(), row(), row(), row(),
            pl.BlockSpec((1, 1, hw), lambda b, h, i: (h, 0, 0)),
        ],
        out_specs=pl.BlockSpec((1, 1, tq, hw), lambda b, h, i: (b, h, i, 0)),
        out_shape=jax.ShapeDtypeStruct((B, H, L, hw), f32),
        compiler_params=_cparams("parallel", "parallel", "parallel"),
        name="diff_attention",
    )(q, k, v, lq1, lk1, lq2, lk2, subw)


def _rmsnorm(x, w):
    return x * lax.rsqrt(jnp.mean(x * x, axis=-1, keepdims=True) + EPS) * w


def _dwconv(x, w, b):
    K, C = w.shape
    y = lax.conv_general_dilated(x, w[:, None, :], window_strides=(1,), padding=[(K // 2, K // 2)],
                                 dimension_numbers=('NWC', 'WIO', 'NWC'), feature_group_count=C)
    return y + b


def _chunk_scan(q, k, v, log_a):
    Bsz, L, H, N = q.shape
    P = v.shape[-1]
    nc = L // CHUNK
    qc = q.reshape(Bsz, nc, CHUNK, H, N)
    kc = k.reshape(Bsz, nc, CHUNK, H, N)
    vc = v.reshape(Bsz, nc, CHUNK, H, P)
    cum = jnp.cumsum(log_a.reshape(Bsz, nc, CHUNK, H), axis=2)
    cum_h = jnp.moveaxis(cum, 3, 2)
    causal = jnp.tril(jnp.ones((CHUNK, CHUNK), dtype=bool))
    seg = jnp.where(causal, cum_h[..., :, None] - cum_h[..., None, :], -jnp.inf)
    scores = jnp.einsum('bcihn,bcjhn->bchij', qc, kc) * jnp.exp(seg)
    y = jnp.einsum('bchij,bcjhp->bcihp', scores, vc)
    to_end = jnp.exp(cum[:, :, -1:, :] - cum)
    chunk_states = jnp.einsum('bcjhn,bcjh,bcjhp->bchnp', kc, to_end, vc)
    chunk_decay = jnp.exp(cum[:, :, -1, :])

    def step(h, inp):
        s, d = inp
        return h * d[..., None, None] + s, h

    h0 = jnp.zeros((Bsz, H, N, P), chunk_states.dtype)
    _, h_prev = lax.scan(step, h0, (jnp.moveaxis(chunk_states, 1, 0), jnp.moveaxis(chunk_decay, 1, 0)))
    h_prev = jnp.moveaxis(h_prev, 0, 1)
    y = y + jnp.einsum('bcihn,bcih,bchnp->bcihp', qc, jnp.exp(cum), h_prev)
    return y.reshape(Bsz, L, H, P)


def _chunk_scan_bwd(q, k, v, log_a):
    Bsz, L, H, N = q.shape
    P = v.shape[-1]
    nc = L // CHUNK
    qc = q.reshape(Bsz, nc, CHUNK, H, N)
    kc = k.reshape(Bsz, nc, CHUNK, H, N)
    vc = v.reshape(Bsz, nc, CHUNK, H, P)
    la = log_a.reshape(Bsz, nc, CHUNK, H)
    cum = jnp.cumsum(la, axis=2)
    ex = cum - la
    ex_h = jnp.moveaxis(ex, 3, 2)
    anti = jnp.triu(jnp.ones((CHUNK, CHUNK), dtype=bool))
    seg = jnp.where(anti, ex_h[..., None, :] - ex_h[..., :, None], -jnp.inf)
    scores = jnp.einsum('bcihn,bcjhn->bchij', qc, kc) * jnp.exp(seg)
    y = jnp.einsum('bchij,bcjhp->bcihp', scores, vc)
    chunk_states = jnp.einsum('bcjhn,bcjh,bcjhp->bchnp', kc, jnp.exp(ex), vc)
    chunk_decay = jnp.exp(cum[:, :, -1, :])

    def step(h, inp):
        s, d = inp
        return h * d[..., None, None] + s, h

    h0 = jnp.zeros((Bsz, H, N, P), chunk_states.dtype)
    _, h_next = lax.scan(step, h0, (jnp.moveaxis(chunk_states, 1, 0), jnp.moveaxis(chunk_decay, 1, 0)), reverse=True)
    h_next = jnp.moveaxis(h_next, 0, 1)
    to_end = jnp.exp(cum[:, :, -1:, :] - ex)
    y = y + jnp.einsum('bcihn,bcih,bchnp->bcihp', qc, to_end, h_next)
    return y.reshape(Bsz, L, H, P)


def _bidir_scan(q, k_fwd, k_bwd, v, la_fwd, la_bwd):
    return _chunk_scan(q, k_fwd, v, la_fwd) + _chunk_scan_bwd(q, k_bwd, v, la_bwd)


def _hyena_filters(L, w1, b1, w2, b2, w3, freq):
    pos = jnp.arange(L, dtype=f32)
    t = pos / (L - 1)
    bands = jnp.linspace(1e-4, HY_BANDS - 1, HY_BANDS, dtype=f32)
    ang = (2.0 * math.pi / L) * pos[:, None] * bands[None, :]
    emb = jnp.concatenate([t[:, None], jnp.cos(ang), -jnp.sin(ang)], axis=-1)
    hid = jnp.sin(freq * (emb @ w1 + b1))
    hid = jnp.sin(freq * (hid @ w2 + b2))
    filt = hid @ w3
    deltas = jnp.abs(jnp.linspace(HY_MIN_DECAY, HY_MAX_DECAY, BR_W, dtype=f32))
    window = jnp.exp(-t[:, None] * jnp.tile(deltas, HY_ORDER * 2)[None, :])
    filt = filt * window
    filt = filt * lax.rsqrt(jnp.sum(filt * filt, axis=0, keepdims=True) + 1e-6)
    return filt.reshape(L, HY_ORDER, 2, BR_W)


def _bidir_fftconv(z, h_fwd, h_bwd, skip):
    L, C = h_fwd.shape
    kern = jnp.concatenate([h_fwd, jnp.zeros((1, C), h_fwd.dtype), h_bwd[:0:-1]], axis=0)
    kf = jnp.fft.rfft(kern, n=2 * L, axis=0)
    zf = jnp.fft.rfft(z, n=2 * L, axis=1)
    y = jnp.fft.irfft(zf * kf[None], n=2 * L, axis=1)[:, :L]
    return y + z * skip


def _hyena_branch(p, W, l):
    L = p.shape[1]
    u = _dwconv(p, W['hy_conv_w'][l], W['hy_conv_b'][l])
    v, x1, x2 = jnp.split(u, 3, axis=-1)
    filt = _hyena_filters(L, W['hf_w1'][l], W['hf_b1'][l], W['hf_w2'][l], W['hf_b2'][l], W['hf_w3'][l], W['hf_freq'][l])
    z = v
    for o, gate in enumerate((x1, x2)):
        z = gate * _bidir_fftconv(z, filt[:, o, 0], filt[:, o, 1], W['hy_bias'][l, o])
    return z


def _retention_branch(p, W, l):
    B, L, _ = p.shape
    qk = RET_HEADS * RET_DK
    q, k, v, g = jnp.split(p, [qk, 2 * qk, 2 * qk + BR_W], axis=-1)
    q = q.reshape(B, L, RET_HEADS, RET_DK)
    k = k.reshape(B, L, RET_HEADS, RET_DK) * (RET_DK ** -0.5)
    v = v.reshape(B, L, RET_HEADS, RET_DV)
    hidx = jnp.arange(RET_HEADS, dtype=f32)
    la_f = jnp.broadcast_to(jnp.log1p(-(2.0 ** (-5.0 - hidx))), (B, L, RET_HEADS))
    la_b = jnp.broadcast_to(jnp.log1p(-(2.0 ** (-5.5 - hidx))), (B, L, RET_HEADS))
    y = _bidir_scan(q, k, k, v, la_f, la_b)
    y = _rmsnorm(y, W['ret_gn_w'][l].reshape(RET_HEADS, RET_DV)).reshape(B, L, BR_W)
    return jax.nn.silu(g) * y


def _ssd_branch(p, dt, W, l):
    B, L, _ = p.shape
    z, xbc = jnp.split(p, [BR_W], axis=-1)
    xbc = jax.nn.silu(_dwconv(xbc, W['ssm_conv_w'][l], W['ssm_conv_b'][l]))
    xs, Bm, Cm = jnp.split(xbc, [BR_W, BR_W + SSM_GROUPS * SSM_N], axis=-1)
    xs = xs.reshape(B, L, SSM_HEADS, SSM_P)
    rep = SSM_HEADS // SSM_GROUPS
    Bh = jnp.repeat(Bm.reshape(B, L, SSM_GROUPS, SSM_N), rep, axis=2)
    Ch = jnp.repeat(Cm.reshape(B, L, SSM_GROUPS, SSM_N), rep, axis=2)
    dt = jax.nn.softplus(dt.reshape(B, L, 2, SSM_HEADS) + W['ssm_dt_bias'][l])
    A = -jnp.exp(W['ssm_A_log'][l])
    y = _bidir_scan(Ch, Bh * dt[:, :, 0, :, None], Bh * dt[:, :, 1, :, None], xs,
                    dt[:, :, 0] * A[0], dt[:, :, 1] * A[1])
    y = y + W['ssm_D'][l][:, None] * xs
    y = y.reshape(B, L, BR_W) * jax.nn.silu(z)
    return _rmsnorm(y, W['ssm_norm_w'][l])


TOKEN_TILE = 512
PEER_TOKEN_TILE = 512
PEER_EXPERT_TILE = SUBLANES * PEER_NK
ATTN_Q_TILE = 256
ATTN_K_TILE = 1024


def _layer_weights(W, l):
    seg = np.kron(np.eye(BR_W // DA_HD), np.ones((DA_HD, DA_HD))) / DA_HD
    nq = 2 * PEER_HEADS * PEER_DK
    return dict(
        nw1=W['norm1_w'][l].reshape(1, D_MODEL),
        nw2=W['norm2_w'][l].reshape(1, D_MODEL),
        w_in=jnp.pad(W['w_in'][l], ((0, 0), (0, IN_COLS_PAD - W['w_in'].shape[-1]))).astype(bf16),
        qw=(jnp.tile(W['da_qnorm_w'][l], BR_W // DA_HD) * (DA_HD ** -0.5)).reshape(1, BR_W),
        kw=jnp.tile(W['da_knorm_w'][l], BR_W // DA_HD).reshape(1, BR_W),
        seg=jnp.asarray(seg, bf16),
        lam=[W[n][l].reshape(1, DA_HD) for n in ('da_lq1', 'da_lk1', 'da_lq2', 'da_lk2')],
        subw=W['da_subln_w'][l].reshape(DA_HEADS, 1, 2 * DA_HD),
        wb=W['w_branch'][l].astype(bf16),
        wg=jnp.transpose(W['w_gate'][l], (1, 0, 2)).reshape(D_MODEL, N_BRANCH * D_MODEL).astype(bf16),
        bg=W['b_gate'][l].reshape(1, N_BRANCH * D_MODEL),
        wo=W['w_out'][l].astype(bf16),
        wq_t=W['peer_wq'][l].T.astype(bf16),
        bq=jnp.broadcast_to(W['peer_bq'][l][:, None], (nq, LANES)),
        keys=W['peer_keys'][l].reshape(2 * PEER_HEADS, PEER_NK, PEER_DK).astype(bf16),
        u=W['peer_u'][l].astype(bf16),
        v_t=W['peer_v'][l].T.astype(bf16),
    )


def _layer(x, ada, W, lw, l):
    p_hy, q, k, v, p_ret, p_ssm, p_dt = _premix(x, ada, lw['nw1'], lw['w_in'], lw['qw'], lw['kw'], lw['seg'], TOKEN_TILE)
    lam_init = 0.8 - 0.6 * math.exp(-0.3 * l)
    L = x.shape[1]
    y_da = _attention(q, k, v, *lw['lam'], lw['subw'], lam_init, min(ATTN_Q_TILE, L), min(ATTN_K_TILE, L))
    y_hy = _hyena_branch(p_hy, W, l)
    y_ret = _retention_branch(p_ret, W, l)
    y_ssm = _ssd_branch(p_ssm, p_dt[..., :2 * SSM_HEADS], W, l)
    x = _merge(x, ada, lw['nw1'], y_hy, y_da, y_ret, y_ssm, lw['wb'], lw['wg'], lw['bg'], lw['wo'], TOKEN_TILE)
    return _peer(x, ada, lw['nw2'], lw['wq_t'], lw['bq'], lw['keys'], lw['u'], lw['v_t'],
                 PEER_TOKEN_TILE, PEER_EXPERT_TILE)


def kernel(x_prompt, x_sample, c_prompt, c_sample, ada_w, ada_b, norm1_w, norm2_w, w_in, hy_conv_w, hy_conv_b,
           hf_w1, hf_b1, hf_w2, hf_b2, hf_w3, hf_freq, hy_bias, da_qnorm_w, da_knorm_w, da_lq1, da_lk1, da_lq2,
           da_lk2, da_subln_w, ret_gn_w, ssm_conv_w, ssm_conv_b, ssm_dt_bias, ssm_A_log, ssm_D, ssm_norm_w,
           w_branch, w_gate, b_gate, w_out, peer_wq, peer_bq, peer_keys, peer_u, peer_v):
    W = dict(norm1_w=norm1_w, norm2_w=norm2_w, w_in=w_in, hy_conv_w=hy_conv_w, hy_conv_b=hy_conv_b,
             hf_w1=hf_w1, hf_b1=hf_b1, hf_w2=hf_w2, hf_b2=hf_b2, hf_w3=hf_w3, hf_freq=hf_freq, hy_bias=hy_bias,
             da_qnorm_w=da_qnorm_w, da_knorm_w=da_knorm_w, da_lq1=da_lq1, da_lk1=da_lk1, da_lq2=da_lq2,
             da_lk2=da_lk2, da_subln_w=da_subln_w, ret_gn_w=ret_gn_w, ssm_conv_w=ssm_conv_w,
             ssm_conv_b=ssm_conv_b, ssm_dt_bias=ssm_dt_bias, ssm_A_log=ssm_A_log, ssm_D=ssm_D,
             ssm_norm_w=ssm_norm_w, w_branch=w_branch, w_gate=w_gate, b_gate=b_gate, w_out=w_out,
             peer_wq=peer_wq, peer_bq=peer_bq, peer_keys=peer_keys, peer_u=peer_u, peer_v=peer_v)
    nb = x_prompt.shape[0]
    ada_all = _ada_table(jnp.concatenate([c_prompt, c_sample], axis=0), ada_w, ada_b)
    xs = [x_prompt, x_sample]
    for l in range(DEPTH):
        lw = _layer_weights(W, l)
        adas = (ada_all[l, :nb][:, None, :], ada_all[l, nb:][:, None, :])
        xs = [_layer(x, ada, W, lw, l) for x, ada in zip(xs, adas)]
    return tuple(xs)
```
